```python
import jax, jax.numpy as jnp
from jax import lax
import numpy as np


D_MODEL = 1024
BATCH = 2
SEQ = 16384
DEPTH = 2

CTX_LEN = 256
GRID_W = 64
EPS = 1e-6

MIX_WIDTH = D_MODEL
A_WIDTH = MIX_WIDTH // 2
A_HEADS = 4
A_DV = A_WIDTH // A_HEADS
A_DK = A_DV
A_QK = A_HEADS * A_DK
A_CHUNK = 64
B_WIDTH = MIX_WIDTH // 4
B_HEADS = 4
B_HEAD_DIM = B_WIDTH // B_HEADS
B_CONV = 4
CONV_LEFT = B_CONV // 2
CONV_RIGHT = B_CONV - 1 - CONV_LEFT
LRU_C = 8.0
C_WIDTH = MIX_WIDTH - A_WIDTH - B_WIDTH
C_GROUPS = 4
C_GROUP_DIM = C_WIDTH // C_GROUPS
IN_SPLITS = (A_QK, 2 * A_QK, 3 * A_QK, 3 * A_QK + A_WIDTH, 3 * A_QK + 2 * A_WIDTH,
             3 * A_QK + 2 * A_WIDTH + B_WIDTH, 3 * A_QK + 2 * A_WIDTH + 2 * B_WIDTH)
IN_WIDTH = 3 * A_QK + 2 * A_WIDTH + 2 * B_WIDTH + C_WIDTH
N_EXPERTS = 256
TOP_K = 8
N_GROUPS = 8
TOPK_GROUPS = 4
EXPERTS_PER_GROUP = N_EXPERTS // N_GROUPS
EXPERT_HIDDEN = D_MODEL // 4
SHARED_HIDDEN = D_MODEL // 4
ROUTE_SCALE = 2.5
MOE_BLOCK = 128

kernel_name = 'hybrid_hgrn2_rglru_fnet_moe_dit'


def _rms_norm(x, g):
    xf = x.astype(jnp.float32)
    y = xf * lax.rsqrt(jnp.mean(xf * xf, axis=-1, keepdims=True) + EPS)
    return (y * g.astype(jnp.float32)).astype(x.dtype)


def _hgrn2_gates(z, lb):
    z = z.astype(jnp.float32)
    logf = jnp.logaddexp(jnp.log(lb), jnp.log1p(-lb) + jax.nn.log_sigmoid(z))
    k = (1.0 - lb) * jax.nn.sigmoid(-z)
    return k, logf


def _hgrn2_chunk_scan(q, k, v, logf, s0):
    Bn, L = q.shape[0], q.shape[1]
    nc = L // A_CHUNK

    def to_chunks(t):
        return t.reshape(Bn, nc, A_CHUNK, A_HEADS, t.shape[-1]).transpose(1, 0, 3, 2, 4)

    causal = jnp.tril(jnp.ones((A_CHUNK, A_CHUNK), bool))[None, None, :, :, None]

    def step(S, inp):
        qc, kc, vc, lfc = inp
        b = jnp.cumsum(lfc, axis=2)
        o_inter = jnp.einsum('bhtk,bhkv->bhtv', qc * jnp.exp(b), S)
        diff = jnp.where(causal, b[:, :, :, None, :] - b[:, :, None, :, :], -jnp.inf)
        scores = jnp.einsum('bhtk,bhsk,bhtsk->bhts', qc, kc, jnp.exp(diff))
        o = o_inter + jnp.einsum('bhts,bhsv->bhtv', scores, vc)
        b_last = b[:, :, -1:, :]
        S = jnp.exp(b_last[:, :, 0, :, None]) * S + jnp.einsum('bhsk,bhsv->bhkv', kc * jnp.exp(b_last - b), vc)
        return S, o

    S, o = lax.scan(step, s0, (to_chunks(q), to_chunks(k), to_chunks(v), to_chunks(logf)))
    o = o.transpose(1, 0, 3, 2, 4).reshape(Bn, L, A_HEADS, -1)
    return o, S


def _hgrn2_direction(q, v, k, logf, reverse):
    flip = (lambda t: jnp.flip(t, axis=1)) if reverse else (lambda t: t)
    s0 = jnp.zeros((q[0].shape[0], A_HEADS, A_DK, A_DV), jnp.float32)
    o_c, s_c = _hgrn2_chunk_scan(flip(q[0]), flip(k[0]), flip(v[0]), flip(logf[0]), s0)
    o_l, _ = _hgrn2_chunk_scan(flip(q[1]), flip(k[1]), flip(v[1]), flip(logf[1]), s_c)
    return flip(o_c), flip(o_l)


def _head_norm_gate(o, g, gain):
    o = o * lax.rsqrt(jnp.mean(o * o, axis=-1, keepdims=True) + EPS) * gain.astype(jnp.float32)
    o = o * jax.nn.silu(g.astype(jnp.float32).reshape(o.shape))
    return o.reshape(o.shape[0], o.shape[1], A_WIDTH)


def _to_colmajor(t):
    Bn, L, W = t.shape
    rows = L // GRID_W
    return t.reshape(Bn, rows, GRID_W, W).transpose(0, 2, 1, 3).reshape(Bn, L, W)


def _from_colmajor(t):
    Bn, L, W = t.shape
    rows = L // GRID_W
    return t.reshape(Bn, GRID_W, rows, W).transpose(0, 2, 1, 3).reshape(Bn, L, W)


def _dwconv_centred(t, w, b):
    y = lax.conv_general_dilated(t, w[:, None, :].astype(t.dtype), window_strides=(1,),
                                 padding=[(CONV_LEFT, CONV_RIGHT)],
                                 dimension_numbers=('NWC', 'WIO', 'NWC'),
                                 feature_group_count=t.shape[-1])
    return y + b.astype(t.dtype)


def _blockdiag(t, w):
    Bn, L, W = t.shape
    y = jnp.einsum('blhi,hij->blhj', t.reshape(Bn, L, B_HEADS, B_HEAD_DIM), w.astype(jnp.float32))
    return y.reshape(Bn, L, W)


def _lin_combine(left, right):
    a_l, b_l = left
    a_r, b_r = right
    return a_l * a_r, a_r * b_l + b_r


def _linear_scan(a, b, h0):
    a_cum, h = lax.associative_scan(_lin_combine, (a, b), axis=1)
    return a_cum * h0[:, None, :] + h


def _rglru_direction(xc, xl, wa, ba, wx, bx, lam, reverse):
    flip = (lambda t: jnp.flip(t, axis=1)) if reverse else (lambda t: t)
    lam = lam.astype(jnp.float32)

    def coeffs(t):
        r = jax.nn.sigmoid(_blockdiag(t, wa) + ba.astype(jnp.float32))
        i = jax.nn.sigmoid(_blockdiag(t, wx) + bx.astype(jnp.float32))
        log_a = -LRU_C * jax.nn.softplus(-lam) * r
        return jnp.exp(log_a), jnp.sqrt(-jnp.expm1(2.0 * log_a)) * (i * t)

    a_c, b_c = coeffs(flip(xc))
    h_c = _linear_scan(a_c, b_c, jnp.zeros((xc.shape[0], B_WIDTH), jnp.float32))
    a_l, b_l = coeffs(flip(xl))
    h_l = _linear_scan(a_l, b_l, h_c[:, -1])
    return flip(h_c), flip(h_l)


def _fourier(u):
    Bn, L, _ = u.shape
    uf = u.astype(jnp.float32).reshape(Bn, L, C_GROUPS, C_GROUP_DIM)
    y = jnp.fft.fft2(uf, axes=(1, 3), norm='ortho').real
    return y.reshape(Bn, L, C_WIDTH)


def _token_mixers(h, hc, w_in, lb, hg_norm, conv_w, conv_b, wa, ba, wx, bx, lam, w_out, with_ctx):
    seg = jnp.split(h @ w_in, IN_SPLITS, axis=-1)
    segc = jnp.split(hc @ w_in, IN_SPLITS, axis=-1)

    def heads(t):
        return t.astype(jnp.float32).reshape(t.shape[0], t.shape[1], A_HEADS, -1)

    lb_h = lb.astype(jnp.float32).reshape(A_HEADS, A_DK)
    qa = (heads(segc[0]), heads(seg[0]))
    va = (heads(segc[3]), heads(seg[3]))
    oa_c = 0.0
    oa_l = 0.0
    for z_idx, rev in ((1, False), (2, True)):
        kc, lfc = _hgrn2_gates(heads(segc[z_idx]), lb_h)
        kl, lfl = _hgrn2_gates(heads(seg[z_idx]), lb_h)
        o_c, o_l = _hgrn2_direction(qa, va, (kc, kl), (lfc, lfl), rev)
        oa_c = oa_c + o_c
        oa_l = oa_l + o_l
    a_l = _head_norm_gate(oa_l, seg[4], hg_norm)

    xbc = _dwconv_centred(segc[5], conv_w, conv_b).astype(jnp.float32)
    xbl = _dwconv_centred(_to_colmajor(seg[5]), conv_w, conv_b).astype(jnp.float32)
    hb_c = 0.0
    hb_l = 0.0
    for d, rev in ((0, False), (1, True)):
        h_c, h_l = _rglru_direction(xbc, xbl, wa[d], ba[d], wx[d], bx[d], lam[d], rev)
        hb_c = hb_c + h_c
        hb_l = hb_l + h_l
    b_l = _from_colmajor(hb_l) * jax.nn.gelu(seg[6].astype(jnp.float32))

    c_l = _fourier(seg[7])

    out_l = jnp.concatenate([a_l, b_l, c_l], axis=-1).astype(h.dtype) @ w_out
    if not with_ctx:
        return out_l, None
    a_c = _head_norm_gate(oa_c, segc[4], hg_norm)
    b_c = hb_c * jax.nn.gelu(segc[6].astype(jnp.float32))
    c_c = _fourier(segc[7])
    out_c = jnp.concatenate([a_c, b_c, c_c], axis=-1).astype(hc.dtype) @ w_out
    return out_l, out_c


def _swiglu(x, wg, wu, wd):
    return (jax.nn.silu(x @ wg) * (x @ wu)) @ wd


def _moe(h, w_router, b_router, w_gate, w_up, w_down, ws_gate, ws_up, ws_down):
    T, D = h.shape
    scores = jax.nn.sigmoid(h.astype(jnp.float32) @ w_router.astype(jnp.float32))
    biased = scores + b_router.astype(jnp.float32)
    grp_score = lax.top_k(biased.reshape(T, N_GROUPS, EXPERTS_PER_GROUP), 2)[0].sum(-1)
    _, grp_idx = lax.top_k(grp_score, TOPK_GROUPS)
    grp_mask = jnp.any(grp_idx[:, :, None] == jnp.arange(N_GROUPS)[None, None, :], axis=1)
    allowed = jnp.repeat(grp_mask, EXPERTS_PER_GROUP, axis=1)
    _, idx = lax.top_k(jnp.where(allowed, biased, -jnp.inf), TOP_K)
    wts = jnp.take_along_axis(scores, idx, axis=1)
    wts = ROUTE_SCALE * wts / jnp.sum(wts, axis=-1, keepdims=True)

    tk = T * TOP_K
    e_flat = idx.reshape(-1).astype(jnp.int32)
    tok_flat = jnp.repeat(jnp.arange(T, dtype=jnp.int32), TOP_K)
    order = jnp.argsort(e_flat)
    e_sorted = e_flat[order]
    tok_sorted = tok_flat[order]
    w_sorted = wts.reshape(-1)[order]
    counts = jnp.bincount(e_flat, length=N_EXPERTS).astype(jnp.int32)
    starts = jnp.cumsum(counts) - counts
    padded = (counts + MOE_BLOCK - 1) // MOE_BLOCK * MOE_BLOCK
    pend = jnp.cumsum(padded)
    pstart = pend - padded
    dest = pstart[e_sorted] + jnp.arange(tk, dtype=jnp.int32) - starts[e_sorted]
    n_blocks = -(-tk // MOE_BLOCK) + N_EXPERTS
    buf_tok = jnp.full((n_blocks * MOE_BLOCK,), T, jnp.int32).at[dest].set(tok_sorted)
    buf_w = jnp.zeros((n_blocks * MOE_BLOCK,), jnp.float32).at[dest].set(w_sorted)
    blk_expert = jnp.minimum(jnp.searchsorted(pend, jnp.arange(n_blocks, dtype=jnp.int32) * MOE_BLOCK, side='right'),
                             N_EXPERTS - 1)
    h_pad = jnp.concatenate([h, jnp.zeros((1, D), h.dtype)], axis=0)

    def block(acc, inp):
        tok, wt, e = inp
        y = _swiglu(h_pad[tok], w_gate[e], w_up[e], w_down[e])
        return acc.at[tok].add(y.astype(jnp.float32) * wt[:, None]), None

    acc, _ = lax.scan(block, jnp.zeros((T + 1, D), jnp.float32),
                      (buf_tok.reshape(n_blocks, MOE_BLOCK), buf_w.reshape(n_blocks, MOE_BLOCK), blk_expert))
    shared = _swiglu(h, ws_gate, ws_up, ws_down)
    return (acc[:T] + shared.astype(jnp.float32)).astype(h.dtype)


def setup_inputs(seed: int = 0) -> dict:
    key = jax.random.key(seed)
    ks = iter(jax.random.split(key, 40))

    def nrm(shape, scale):
        return jax.random.normal(next(ks), shape, jnp.float32) * scale

    D = D_MODEL
    L = DEPTH
    a0 = jax.random.uniform(next(ks), (L, 2, B_WIDTH), jnp.float32, 0.9, 0.999)
    s = a0 ** (1.0 / LRU_C)
    lam = jnp.log(s) - jnp.log1p(-s)
    return {
        'x': nrm((BATCH, SEQ, D), 1.0),
        'c': nrm((BATCH, D), 1.0),
        'ctx': nrm((BATCH, CTX_LEN, D), 1.0),
        'c_ctx': nrm((D,), 1.0),
        'w_mod': nrm((L, D, 6 * D), 0.5 * D ** -0.5),
        'b_mod': nrm((L, 6 * D), 0.02),
        'norm_mix': 1.0 + nrm((L, D), 0.02),
        'norm_ffn': 1.0 + nrm((L, D), 0.02),
        'w_in': nrm((L, D, IN_WIDTH), D ** -0.5),
        'hg_lb': nrm((L, A_QK), 0.5),
        'hg_norm': 1.0 + nrm((L, A_DV), 0.02),
        'lru_conv_w': nrm((L, B_CONV, B_WIDTH), B_CONV ** -0.5),
        'lru_conv_b': nrm((L, B_WIDTH), 0.02),
        'lru_wa': nrm((L, 2, B_HEADS, B_HEAD_DIM, B_HEAD_DIM), B_HEAD_DIM ** -0.5),
        'lru_ba': nrm((L, 2, B_WIDTH), 0.02),
        'lru_wx': nrm((L, 2, B_HEADS, B_HEAD_DIM, B_HEAD_DIM), B_HEAD_DIM ** -0.5),
        'lru_bx': nrm((L, 2, B_WIDTH), 0.02),
        'lru_lam': lam,
        'w_out': nrm((L, MIX_WIDTH, D), MIX_WIDTH ** -0.5),
        'w_router': nrm((L, D, N_EXPERTS), D ** -0.5),
        'b_router': nrm((L, N_EXPERTS), 0.01),
        'w_exp_gate': nrm((L, N_EXPERTS, D, EXPERT_HIDDEN), D ** -0.5),
        'w_exp_up': nrm((L, N_EXPERTS, D, EXPERT_HIDDEN), D ** -0.5),
        'w_exp_down': nrm((L, N_EXPERTS, EXPERT_HIDDEN, D), EXPERT_HIDDEN ** -0.5),
        'w_sh_gate': nrm((L, D, SHARED_HIDDEN), D ** -0.5),
        'w_sh_up': nrm((L, D, SHARED_HIDDEN), D ** -0.5),
        'w_sh_down': nrm((L, SHARED_HIDDEN, D), SHARED_HIDDEN ** -0.5),
        'norm_final': 1.0 + nrm((D,), 0.02),
    }


def reference(x, c, ctx, c_ctx, w_mod, b_mod, norm_mix, norm_ffn, w_in, hg_lb, hg_norm,
              lru_conv_w, lru_conv_b, lru_wa, lru_ba, lru_wx, lru_bx, lru_lam, w_out,
              w_router, b_router, w_exp_gate, w_exp_up, w_exp_down, w_sh_gate, w_sh_up, w_sh_down,
              norm_final):
    D = x.shape[-1]
    lb_all = jnp.cumsum(jax.nn.softmax(hg_lb.astype(jnp.float32), axis=0), axis=0)
    lb_all = lb_all - lb_all[0:1]
    xc = ctx
    for l in range(DEPTH):
        last = l == DEPTH - 1
        mod = (jax.nn.silu(c) @ w_mod[l] + b_mod[l])[:, None, :]
        mod_c = (jax.nn.silu(c_ctx) @ w_mod[l] + b_mod[l])[None, None, :]
        sh1, sc1, g1, sh2, sc2, g2 = jnp.split(mod, 6, axis=-1)
        csh1, csc1, cg1, csh2, csc2, cg2 = jnp.split(mod_c, 6, axis=-1)

        h = _rms_norm(x, norm_mix[l]) * (1.0 + sc1) + sh1
        hc = _rms_norm(xc, norm_mix[l]) * (1.0 + csc1) + csh1
        mix, mix_c = _token_mixers(h, hc, w_in[l], lb_all[l], hg_norm[l], lru_conv_w[l], lru_conv_b[l],
                                   lru_wa[l], lru_ba[l], lru_wx[l], lru_bx[l], lru_lam[l], w_out[l],
                                   not last)
        x = x + g1 * mix
        h2 = _rms_norm(x, norm_ffn[l]) * (1.0 + sc2) + sh2
        moe_args = (w_router[l], b_router[l], w_exp_gate[l], w_exp_up[l], w_exp_down[l],
                    w_sh_gate[l], w_sh_up[l], w_sh_down[l])
        if last:
            x = x + g2 * _moe(h2.reshape(-1, D), *moe_args).reshape(x.shape)
        else:
            xc = xc + cg1 * mix_c
            h2c = _rms_norm(xc, norm_ffn[l]) * (1.0 + csc2) + csh2
            n_lat = x.shape[0] * x.shape[1]
            y = _moe(jnp.concatenate([h2.reshape(-1, D), h2c.reshape(-1, D)], axis=0), *moe_args)
            x = x + g2 * y[:n_lat].reshape(x.shape)
            xc = xc + cg2 * y[n_lat:].reshape(xc.shape)
    return _rms_norm(x, norm_final)
```

```python
import functools
import math

import numpy as np
import jax
import jax.numpy as jnp
from jax import lax
from jax.experimental import pallas as pl
from jax.experimental.pallas import tpu as pltpu

F32 = jnp.float32
BF16 = jnp.bfloat16

D = 1024
EPS = 1e-6
GRID_W = 64
A_HEADS = 4
A_DK = 128
A_WIDTH = 512
CHUNK = 64
SUB = 16
B_WIDTH = 256
B_HEADS = 4
B_HEAD_DIM = 64
LRU_C = 8.0
C_WIDTH = 256
C_GROUPS = 4
C_GROUP_DIM = 64
FFT_N2 = 128
SEG_WIDTHS = (512, 512, 512, 512, 512, 256, 256, 256)
IN_WIDTH = sum(SEG_WIDTHS)
N_EXPERTS = 256
TOP_K = 8
N_GROUPS = 8
TOPK_GROUPS = 4
EXPERTS_PER_GROUP = 32
EXPERT_HIDDEN = 256
ROUTE_SCALE = 2.5
MOE_ROWS = 256

ROW_TILE = 256
VMEM_LIMIT = 56 * 1024 * 1024


def _cparams(*sem):
    return pltpu.CompilerParams(dimension_semantics=sem, vmem_limit_bytes=VMEM_LIMIT)


def _bdot(a, b):
    return jnp.dot(a.astype(BF16), b.astype(BF16), preferred_element_type=F32)


def _bdot_nt(a, b):
    return lax.dot_general(a.astype(BF16), b.astype(BF16), (((1,), (1,)), ((), ())),
                           preferred_element_type=F32)


def _bdot_tn(a, b):
    return lax.dot_general(a.astype(BF16), b.astype(BF16), (((0,), (0,)), ((), ())),
                           preferred_element_type=F32)


def _split_bf16(a):
    hi = a.astype(BF16)
    lo = (a - hi.astype(F32)).astype(BF16)
    return hi, lo


def _sigmoid(x):
    return jax.nn.sigmoid(x)


def _rms(x, g):
    return x * lax.rsqrt(jnp.mean(x * x, axis=-1, keepdims=True) + EPS) * g


def _mod_kernel(c_ref, w_ref, b_ref, o_ref):
    c = c_ref[...]
    o_ref[0] = _bdot(c * _sigmoid(c), w_ref[0]) + b_ref[0]


def _modulation(cc, w_mod, b_mod):
    n_layers = w_mod.shape[0]
    return pl.pallas_call(
        _mod_kernel,
        grid=(n_layers, 6),
        in_specs=[pl.BlockSpec((8, D), lambda l, j: (0, 0)),
                  pl.BlockSpec((1, D, D), lambda l, j: (l, 0, j)),
                  pl.BlockSpec((1, 1, D), lambda l, j: (l, 0, j))],
        out_specs=pl.BlockSpec((1, 8, D), lambda l, j: (l, 0, j)),
        out_shape=jax.ShapeDtypeStruct((n_layers, 8, 6 * D), F32),
        compiler_params=_cparams("arbitrary", "arbitrary"),
        name="modulation",
    )(cc, w_mod, b_mod.reshape(n_layers, 1, 6 * D))


def _inproj_kernel(x_ref, mods_ref, g_ref, w_ref, *out_refs, ctx_blk):
    b = pl.program_id(0)
    j = pl.program_id(1)
    row = jnp.where(j == ctx_blk, 0, b + 1)
    m = mods_ref[pl.ds(row, 1), :]
    h = _rms(x_ref[0], g_ref[...]) * (1.0 + m[:, D:2 * D]) + m[:, 0:D]
    hb = h.astype(BF16)
    lo = 0
    for o_ref, w in zip(out_refs, SEG_WIDTHS):
        o_ref[0] = jnp.dot(hb, w_ref[:, lo:lo + w], preferred_element_type=F32)
        lo += w


def _inproj(x_all, mods, g, w_in_bf, n_lat):
    bsz, nt, _ = x_all.shape
    nblk = nt // ROW_TILE
    return pl.pallas_call(
        functools.partial(_inproj_kernel, ctx_blk=n_lat // ROW_TILE),
        grid=(bsz, nblk),
        in_specs=[pl.BlockSpec((1, ROW_TILE, D), lambda b, j: (b, j, 0)),
                  pl.BlockSpec((8, 6 * D), lambda b, j: (0, 0)),
                  pl.BlockSpec((1, D), lambda b, j: (0, 0)),
                  pl.BlockSpec((D, IN_WIDTH), lambda b, j: (0, 0))],
        out_specs=[pl.BlockSpec((1, ROW_TILE, w), lambda b, j: (b, j, 0)) for w in SEG_WIDTHS],
        out_shape=[jax.ShapeDtypeStruct((bsz, nt, w), F32) for w in SEG_WIDTHS],
        compiler_params=_cparams("arbitrary", "arbitrary"),
        name="inproj",
    )(x_all, mods, g.reshape(1, D), w_in_bf)


def _hgrn2_gates(z, lbc, lb_zero):
    e = jnp.exp(-jnp.abs(z))
    lsig = jnp.minimum(z, 0.0) - jnp.log1p(e)
    r = 1.0 / (1.0 + e)
    sig_neg = jnp.where(z > 0, e * r, r)
    if lb_zero:
        return sig_neg, lsig
    loglb, log1mlb, one_m_lb = lbc
    c = log1mlb + lsig
    m = jnp.maximum(loglb, c)
    logf = m + jnp.log(jnp.exp(loglb - m) + jnp.exp(c - m))
    return one_m_lb * sig_neg, logf


def _hgrn2_chunk(q, k, v, logf, st, reverse):
    ri = lax.broadcasted_iota(jnp.int32, (CHUNK, CHUNK), 0)
    ci = lax.broadcasted_iota(jnp.int32, (CHUNK, CHUNK), 1)
    tri = jnp.where((ci >= ri) if reverse else (ci <= ri), 1.0, 0.0).astype(BF16)
    hi, lo = _split_bf16(logf)
    b = (jnp.dot(tri, hi, preferred_element_type=F32)
         + jnp.dot(tri, lo, preferred_element_type=F32))
    end = 0 if reverse else CHUNK - 1
    b_end = b[end:end + 1]
    o_inter = _bdot_nt(q * jnp.exp(b), st)
    st_new = st * jnp.exp(b_end) + _bdot_tn(v, k * jnp.exp(b_end - b))

    row = lax.broadcasted_iota(jnp.int32, (SUB, CHUNK), 0)
    col = lax.broadcasted_iota(jnp.int32, (SUB, CHUNK), 1)
    srow = lax.broadcasted_iota(jnp.int32, (CHUNK, A_DK), 0)
    outs = []
    for i in range(CHUNK // SUB):
        r0 = i * SUB
        if reverse:
            bref = b[r0 + SUB:r0 + SUB + 1] if r0 + SUB < CHUNK else jnp.zeros_like(b_end)
            in_range = srow >= r0
            keep = col >= row + r0
        else:
            bref = b[r0 - 1:r0] if r0 > 0 else jnp.zeros_like(b_end)
            in_range = srow < r0 + SUB
            keep = col <= row + r0
        qt = q[r0:r0 + SUB] * jnp.exp(b[r0:r0 + SUB] - bref)
        kt = k * jnp.exp(jnp.where(in_range, bref - b, -jnp.inf))
        a = jnp.where(keep, _bdot_nt(qt, kt), 0.0)
        outs.append(_bdot(a, v))
    return jnp.concatenate(outs, axis=0) + o_inter, st_new


def _hgrn2_kernel(*refs, reverse, finalize, lb_zero, rows):
    if finalize:
        q_ref, z_ref, v_ref, lbc_ref, oprev_ref, g_ref, gain_ref, o_ref, st_ref = refs
    else:
        q_ref, z_ref, v_ref, lbc_ref, o_ref, st_ref = refs

    @pl.when(pl.program_id(1) == 0)
    def _():
        st_ref[...] = jnp.zeros_like(st_ref)

    nch = rows // CHUNK

    def chunk_body(ci, carry):
        c = (nch - 1 - ci) if reverse else ci
        r0 = pl.multiple_of(c * CHUNK, CHUNK)
        for h in range(A_HEADS):
            cols = slice(h * A_DK, (h + 1) * A_DK)
            q = q_ref[0, pl.ds(r0, CHUNK), cols]
            z = z_ref[0, pl.ds(r0, CHUNK), cols]
            v = v_ref[0, pl.ds(r0, CHUNK), cols]
            lbc = (lbc_ref[0:1, cols], lbc_ref[1:2, cols], lbc_ref[2:3, cols])
            k, logf = _hgrn2_gates(z, lbc, lb_zero)
            o, st_new = _hgrn2_chunk(q, k, v, logf, st_ref[h], reverse)
            st_ref[h] = st_new
            if finalize:
                o = o + oprev_ref[0, pl.ds(r0, CHUNK), cols]
                o = _rms(o, gain_ref[...])
                g = g_ref[0, pl.ds(r0, CHUNK), cols]
                o = o * (g * _sigmoid(g))
            o_ref[0, pl.ds(r0, CHUNK), cols] = o
        return carry

    lax.fori_loop(0, nch, chunk_body, 0)


def _hgrn2(q, z, v, lbc, n_lat, *, reverse, lb_zero, oprev=None, g=None, gain=None):
    bsz, nt, _ = q.shape
    nblk = nt // ROW_TILE
    ctx_blk = n_lat // ROW_TILE
    finalize = oprev is not None

    if reverse:
        def pos(j):
            return jnp.where(j == 0, ctx_blk, ctx_blk - j)
    else:
        def pos(j):
            return jnp.where(j == 0, ctx_blk, j - 1)

    seq_spec = pl.BlockSpec((1, ROW_TILE, A_WIDTH), lambda b, j: (b, pos(j), 0))
    in_specs = [seq_spec, seq_spec, seq_spec, pl.BlockSpec((8, A_WIDTH), lambda b, j: (0, 0))]
    args = [q, z, v, lbc]
    if finalize:
        in_specs += [seq_spec, seq_spec, pl.BlockSpec((1, A_DK), lambda b, j: (0, 0))]
        args += [oprev, g, gain.reshape(1, A_DK)]
    return pl.pallas_call(
        functools.partial(_hgrn2_kernel, reverse=reverse, finalize=finalize, lb_zero=lb_zero,
                          rows=ROW_TILE),
        grid=(bsz, nblk),
        in_specs=in_specs,
        out_specs=seq_spec,
        out_shape=jax.ShapeDtypeStruct((bsz, nt, A_WIDTH), F32),
        scratch_shapes=[pltpu.VMEM((A_HEADS, A_DK, A_DK), F32)],
        compiler_params=_cparams("arbitrary", "arbitrary"),
        name="hgrn2_bwd" if reverse else "hgrn2_fwd",
    )(*args)


LRU_LANES = 128
LRU_RCHUNK = 32


def _shift_down(t, n=1):
    return jnp.concatenate([jnp.zeros((n, t.shape[1]), t.dtype), t[:-n]], axis=0)


def _shift_up(t, n=1):
    return jnp.concatenate([t[n:], jnp.zeros((n, t.shape[1]), t.dtype)], axis=0)


def _rglru_kernel(xl_ref, xc_ref, cw_ref, cb_ref, wa_ref, wx_ref, ba_ref, bx_ref, lam_ref,
                  hl_ref, hc_ref, xcv_ref, a_ref, bc_ref, pe_ref, ac_ref, bcc_ref, *, n_rows, n_ctx):
    cw = [cw_ref[i:i + 1, :] for i in range(4)]
    cb = cb_ref[...]

    def taps(t0, t1, t2, t3):
        return cw[0] * t0 + cw[1] * t1 + cw[2] * t2 + cw[3] * t3 + cb

    def coeffs(xv, d):
        ga = _bdot(xv, wa_ref[d]) + ba_ref[d:d + 1, :]
        gx = _bdot(xv, wx_ref[d]) + bx_ref[d:d + 1, :]
        lam = lam_ref[d:d + 1, :]
        softplus_neg = jnp.maximum(-lam, 0.0) + jnp.log1p(jnp.exp(-jnp.abs(lam)))
        log_a = (-LRU_C * softplus_neg) * _sigmoid(ga)
        a = jnp.exp(log_a)
        th = jnp.tanh(log_a)
        return a, jnp.sqrt(-2.0 * th / (1.0 - th)) * (_sigmoid(gx) * xv)

    R = n_rows
    xcv_ref[0] = taps(_shift_down(xl_ref[0, R - 2]), _shift_down(xl_ref[0, R - 1]),
                      xl_ref[0, 0], xl_ref[0, 1])
    xcv_ref[1] = taps(_shift_down(xl_ref[0, R - 1]), xl_ref[0, 0], xl_ref[0, 1], xl_ref[0, 2])

    def conv_body(r, carry):
        xcv_ref[r] = taps(xl_ref[0, r - 2], xl_ref[0, r - 1], xl_ref[0, r], xl_ref[0, r + 1])
        return carry

    lax.fori_loop(2, R - 1, conv_body, 0)
    xcv_ref[R - 1] = taps(xl_ref[0, R - 3], xl_ref[0, R - 2], xl_ref[0, R - 1],
                          _shift_up(xl_ref[0, 0]))

    xc = xc_ref[0]
    xcc = taps(_shift_down(xc, 2), _shift_down(xc, 1), xc, _shift_up(xc, 1))
    h0 = []
    for d in range(2):
        a, bco = coeffs(xcc, d)
        ac_ref[...] = a
        bcc_ref[...] = bco

        def ctx_body(i, h, d=d):
            t = (n_ctx - 1 - i) if d else i
            h = ac_ref[pl.ds(t, 1), :] * h + bcc_ref[pl.ds(t, 1), :]
            if d:
                hc_ref[0, pl.ds(t, 1), :] = hc_ref[0, pl.ds(t, 1), :] + h
            else:
                hc_ref[0, pl.ds(t, 1), :] = h
            return h

        h0.append(lax.fori_loop(0, n_ctx, ctx_body, jnp.zeros((1, LRU_LANES), F32)))

    n_chunks = R // LRU_RCHUNK
    rows_per_chunk = LRU_RCHUNK * GRID_W

    def load_coeffs(ci, d):
        c0 = pl.multiple_of(ci * LRU_RCHUNK, LRU_RCHUNK)
        xv = xcv_ref[pl.ds(c0, LRU_RCHUNK)].reshape(rows_per_chunk, LRU_LANES)
        a, bco = coeffs(xv, d)
        a_ref[...] = a.reshape(LRU_RCHUNK, GRID_W, LRU_LANES)
        bc_ref[...] = bco.reshape(LRU_RCHUNK, GRID_W, LRU_LANES)
        return c0

    for d in range(2):
        def order(i, n, d=d):
            return (n - 1 - i) if d else i

        def p1_chunk(i, pe, d=d):
            load_coeffs(order(i, n_chunks), d)

            def p1_row(k, pe):
                p, e = pe
                r = order(k, LRU_RCHUNK)
                a = a_ref[r]
                return p * a, a * e + bc_ref[r]

            return lax.fori_loop(0, LRU_RCHUNK, p1_row, pe)

        p, e = lax.fori_loop(0, n_chunks, p1_chunk,
                             (jnp.ones((GRID_W, LRU_LANES), F32), jnp.zeros((GRID_W, LRU_LANES), F32)))
        pe_ref[0] = p
        pe_ref[1] = e

        def carry_body(i, c, d=d):
            w = order(i, GRID_W)
            pe_ref[2, pl.ds(w, 1), :] = c
            return pe_ref[0, pl.ds(w, 1), :] * c + pe_ref[1, pl.ds(w, 1), :]

        lax.fori_loop(0, GRID_W, carry_body, h0[d])

        def p2_chunk(i, h, d=d):
            c0 = load_coeffs(order(i, n_chunks), d)

            def p2_row(k, h):
                r = order(k, LRU_RCHUNK)
                h = a_ref[r] * h + bc_ref[r]
                if d:
                    hl_ref[0, c0 + r] = hl_ref[0, c0 + r] + h
                else:
                    hl_ref[0, c0 + r] = h
                return h

            return lax.fori_loop(0, LRU_RCHUNK, p2_row, h)

        lax.fori_loop(0, n_chunks, p2_chunk, pe_ref[2])


def _rglru(xb_all, n_lat, conv_w, conv_b, wa_bd, wx_bd, ba, bx, lam):
    bsz, nt, _ = xb_all.shape
    n_ctx = nt - n_lat
    n_rows = n_lat // GRID_W
    x4 = xb_all.reshape(bsz, nt // GRID_W, GRID_W, B_WIDTH)
    nh = B_WIDTH // LRU_LANES
    hl, hc = pl.pallas_call(
        functools.partial(_rglru_kernel, n_rows=n_rows, n_ctx=n_ctx),
        grid=(bsz, nh),
        in_specs=[pl.BlockSpec((1, n_rows, GRID_W, LRU_LANES), lambda b, h: (b, 0, 0, h)),
                  pl.BlockSpec((1, n_ctx, LRU_LANES), lambda b, h: (b, n_lat // n_ctx, h)),
                  pl.BlockSpec((4, LRU_LANES), lambda b, h: (0, h)),
                  pl.BlockSpec((1, LRU_LANES), lambda b, h: (0, h)),
                  pl.BlockSpec((2, LRU_LANES, LRU_LANES), lambda b, h: (0, h, h)),
                  pl.BlockSpec((2, LRU_LANES, LRU_LANES), lambda b, h: (0, h, h)),
                  pl.BlockSpec((2, LRU_LANES), lambda b, h: (0, h)),
                  pl.BlockSpec((2, LRU_LANES), lambda b, h: (0, h)),
                  pl.BlockSpec((2, LRU_LANES), lambda b, h: (0, h))],
        out_specs=[pl.BlockSpec((1, n_rows, GRID_W, LRU_LANES), lambda b, h: (b, 0, 0, h)),
                   pl.BlockSpec((1, n_ctx, LRU_LANES), lambda b, h: (b, 0, h))],
        out_shape=[jax.ShapeDtypeStruct((bsz, n_rows, GRID_W, B_WIDTH), F32),
                   jax.ShapeDtypeStruct((bsz, n_ctx, B_WIDTH), F32)],
        scratch_shapes=[pltpu.VMEM((n_rows, GRID_W, LRU_LANES), F32),
                        pltpu.VMEM((LRU_RCHUNK, GRID_W, LRU_LANES), F32),
                        pltpu.VMEM((LRU_RCHUNK, GRID_W, LRU_LANES), F32),
                        pltpu.VMEM((3, GRID_W, LRU_LANES), F32),
                        pltpu.VMEM((n_ctx, LRU_LANES), F32),
                        pltpu.VMEM((n_ctx, LRU_LANES), F32)],
        compiler_params=_cparams("arbitrary", "arbitrary"),
        name="rglru",
    )(x4, xb_all, conv_w, conv_b.reshape(1, B_WIDTH), wa_bd, wx_bd, ba, bx, lam)
    return hl.reshape(bsz, n_lat, B_WIDTH), hc


def _dft_cos_sin(n):
    k = np.arange(n)
    ang = 2.0 * np.pi * ((k[:, None] * k[None, :]) % n) / n
    return np.cos(ang), np.sin(ang)


def _channel_dft():
    c, s = _dft_cos_sin(C_GROUP_DIM)
    eye = np.eye(C_GROUPS)
    return np.kron(eye, c), np.kron(eye, s)


def _fft1_kernel(f_ref, u_ref, o_ref):
    o_ref[0] = _bdot(f_ref[...], u_ref[0])


def _fft2_kernel(a_ref, twr_ref, twi_ref, cc_ref, sc_ref, c2_ref, s2_ref, o_ref, *, scale, group):
    ar = a_ref[0, 0]
    ai = a_ref[0, 1]
    twr = twr_ref[...]
    twi = twi_ref[...]
    rows = group * FFT_N2
    tr = (ar * twr - ai * twi).reshape(rows, C_WIDTH)
    ti = (ar * twi + ai * twr).reshape(rows, C_WIDTH)
    cc = cc_ref[...]
    sc = sc_ref[...]
    br = _bdot(tr, cc) + _bdot(ti, sc)
    bi = _bdot(ti, cc) - _bdot(tr, sc)
    for j in range(group):
        rs = slice(j * FFT_N2, (j + 1) * FFT_N2)
        xr = _bdot(c2_ref[...], br[rs]) + _bdot(s2_ref[...], bi[rs])
        o_ref[0, :, j, :] = xr * scale


def _fourier_latent(u_all, n_lat):
    bsz, nt, _ = u_all.shape
    n1 = n_lat // FFT_N2
    cols = FFT_N2 * C_WIDTH
    col_tile = min(cols, 4096)
    c1, s1 = _dft_cos_sin(n1)
    f1 = jnp.asarray(np.concatenate([c1, -s1], axis=0), BF16)
    u3 = u_all.reshape(bsz, nt // FFT_N2, cols)
    a = pl.pallas_call(
        _fft1_kernel,
        grid=(bsz, cols // col_tile),
        in_specs=[pl.BlockSpec((2 * n1, n1), lambda b, j: (0, 0)),
                  pl.BlockSpec((1, n1, col_tile), lambda b, j: (b, 0, j))],
        out_specs=pl.BlockSpec((1, 2 * n1, col_tile), lambda b, j: (b, 0, j)),
        out_shape=jax.ShapeDtypeStruct((bsz, 2 * n1, cols), F32),
        compiler_params=_cparams("arbitrary", "arbitrary"),
        name="fourier_stage1",
    )(f1, u3)

    group = 8
    k1 = np.arange(n1)[:, None]
    n2 = np.arange(FFT_N2)[None, :]
    ang = 2.0 * np.pi * ((k1 * n2) % n_lat) / n_lat
    twr = jnp.broadcast_to(jnp.asarray(np.cos(ang), F32)[:, :, None], (n1, FFT_N2, C_WIDTH))
    twi = jnp.broadcast_to(jnp.asarray(-np.sin(ang), F32)[:, :, None], (n1, FFT_N2, C_WIDTH))
    cc, sc = _channel_dft()
    c2, s2 = _dft_cos_sin(FFT_N2)
    scale = 1.0 / math.sqrt(n_lat * C_GROUP_DIM)
    a5 = a.reshape(bsz, 2, n1, FFT_N2, C_WIDTH)
    const = lambda shape: pl.BlockSpec(shape, lambda b, g: (0,) * len(shape))
    out = pl.pallas_call(
        functools.partial(_fft2_kernel, scale=scale, group=group),
        grid=(bsz, n1 // group),
        in_specs=[pl.BlockSpec((1, 2, group, FFT_N2, C_WIDTH), lambda b, g: (b, 0, g, 0, 0)),
                  pl.BlockSpec((group, FFT_N2, C_WIDTH), lambda b, g: (g, 0, 0)),
                  pl.BlockSpec((group, FFT_N2, C_WIDTH), lambda b, g: (g, 0, 0)),
                  const((C_WIDTH, C_WIDTH)), const((C_WIDTH, C_WIDTH)),
                  const((FFT_N2, FFT_N2)), const((FFT_N2, FFT_N2))],
        out_specs=pl.BlockSpec((1, FFT_N2, group, C_WIDTH), lambda b, g: (b, 0, g, 0)),
        out_shape=jax.ShapeDtypeStruct((bsz, FFT_N2, n1, C_WIDTH), F32),
        compiler_params=_cparams("arbitrary", "arbitrary"),
        name="fourier_stage2",
    )(a5, twr, twi, jnp.asarray(cc, BF16), jnp.asarray(sc, BF16),
      jnp.asarray(c2, BF16), jnp.asarray(s2, BF16))
    return out.reshape(bsz, n_lat, C_WIDTH)


def _fft_ctx_kernel(u_ref, cc_ref, sc_ref, cm_ref, sm_ref, o_ref, *, scale):
    u = u_ref[0]
    zr = _bdot(u, cc_ref[...])
    zi = -_bdot(u, sc_ref[...])
    o_ref[0] = (_bdot(cm_ref[...], zr) + _bdot(sm_ref[...], zi)) * scale


def _fourier_ctx(u_all, n_lat):
    bsz, nt, _ = u_all.shape
    n_ctx = nt - n_lat
    cc, sc = _channel_dft()
    cm, sm = _dft_cos_sin(n_ctx)
    const = lambda shape: pl.BlockSpec(shape, lambda b: (0,) * len(shape))
    return pl.pallas_call(
        functools.partial(_fft_ctx_kernel, scale=1.0 / math.sqrt(n_ctx * C_GROUP_DIM)),
        grid=(bsz,),
        in_specs=[pl.BlockSpec((1, n_ctx, C_WIDTH), lambda b: (b, n_lat // n_ctx, 0)),
                  const((C_WIDTH, C_WIDTH)), const((C_WIDTH, C_WIDTH)),
                  const((n_ctx, n_ctx)), const((n_ctx, n_ctx))],
        out_specs=pl.BlockSpec((1, n_ctx, C_WIDTH), lambda b: (b, 0, 0)),
        out_shape=jax.ShapeDtypeStruct((bsz, n_ctx, C_WIDTH), F32),
        compiler_params=_cparams("arbitrary"),
        name="fourier_ctx",
    )(u_all, jnp.asarray(cc, BF16), jnp.asarray(sc, BF16), jnp.asarray(cm, BF16), jnp.asarray(sm, BF16))


def _outproj_kernel(x_ref, a_ref, hbl_ref, hbc_ref, yb_ref, cl_ref, cc_ref, mods_ref, g_ref,
                    wout_ref, wrh_ref, wrl_ref, xn_ref, h2_ref, lg_ref, *, ctx_blk):
    b = pl.program_id(0)
    j = pl.program_id(1)
    is_ctx = j == ctx_blk
    row = jnp.where(is_ctx, 0, b + 1)
    m = mods_ref[pl.ds(row, 1), :]
    hb = jnp.where(is_ctx, hbc_ref[0], hbl_ref[0])
    cf = jnp.where(is_ctx, cc_ref[0], cl_ref[0])
    bpart = hb * jax.nn.gelu(yb_ref[0])
    mix = (_bdot(a_ref[0], wout_ref[0:A_WIDTH, :])
           + _bdot(bpart, wout_ref[A_WIDTH:A_WIDTH + B_WIDTH, :])
           + _bdot(cf, wout_ref[A_WIDTH + B_WIDTH:, :]))
    xn = x_ref[0] + m[:, 2 * D:3 * D] * mix
    xn_ref[0] = xn
    h2 = _rms(xn, g_ref[...]) * (1.0 + m[:, 4 * D:5 * D]) + m[:, 3 * D:4 * D]
    h2_ref[0] = h2
    hi, lo = _split_bf16(h2)
    wrh = wrh_ref[...]
    lg_ref[0] = (jnp.dot(hi, wrh, preferred_element_type=F32)
                 + jnp.dot(lo, wrh, preferred_element_type=F32)
                 + jnp.dot(hi, wrl_ref[...], preferred_element_type=F32))


def _outproj(x_all, a_all, hb_lat, hb_ctx, yb_all, c_lat, c_ctx, mods, g, w_out_bf, wr_hi, wr_lo, n_lat):
    bsz, nt, _ = x_all.shape
    nblk = nt // ROW_TILE
    ctx_blk = n_lat // ROW_TILE
    n_ctx = nt - n_lat
    assert n_ctx == ROW_TILE
    row = lambda w: pl.BlockSpec((1, ROW_TILE, w), lambda b, j: (b, j, 0))
    lat = lambda w: pl.BlockSpec((1, ROW_TILE, w), lambda b, j: (b, jnp.minimum(j, ctx_blk - 1), 0))
    ctx = lambda w: pl.BlockSpec((1, ROW_TILE, w), lambda b, j: (b, 0, 0))
    const = lambda shape: pl.BlockSpec(shape, lambda b, j: (0,) * len(shape))
    return pl.pallas_call(
        functools.partial(_outproj_kernel, ctx_blk=ctx_blk),
        grid=(bsz, nblk),
        in_specs=[row(D), row(A_WIDTH), lat(B_WIDTH), ctx(B_WIDTH), row(B_WIDTH), lat(C_WIDTH),
                  ctx(C_WIDTH), const((8, 6 * D)), const((1, D)), const((D, D)),
                  const((D, N_EXPERTS)), const((D, N_EXPERTS))],
        out_specs=[row(D), row(D), row(N_EXPERTS)],
        out_shape=[jax.ShapeDtypeStruct((bsz, nt, D), F32),
                   jax.ShapeDtypeStruct((bsz, nt, D), F32),
                   jax.ShapeDtypeStruct((bsz, nt, N_EXPERTS), F32)],
        compiler_params=_cparams("arbitrary", "arbitrary"),
        name="outproj",
    )(x_all, a_all, hb_lat, hb_ctx, yb_all, c_lat, c_ctx, mods, g.reshape(1, D), w_out_bf, wr_hi, wr_lo)


def _route_kernel(lg_ref, bias_ref, idx_ref, wts_ref, rank_ref, cnt_ref, base_ref, *, tile):
    @pl.when(pl.program_id(0) == 0)
    def _():
        base_ref[...] = jnp.zeros_like(base_ref)

    neg = -jnp.inf
    scores = _sigmoid(lg_ref[...])
    biased = scores + bias_ref[...]
    lane_i = lax.broadcasted_iota(jnp.int32, (tile, N_EXPERTS), 1)
    lane = lane_i.astype(F32)
    grp = lane_i // EXPERTS_PER_GROUP

    def first_argmax(vals):
        m = jnp.max(vals, axis=-1, keepdims=True)
        i = jnp.min(jnp.where(vals == m, lane, float(N_EXPERTS)), axis=-1, keepdims=True)
        return m, i

    gs = []
    for g in range(N_GROUPS):
        vals = jnp.where(grp == g, biased, neg)
        m1, i1 = first_argmax(vals)
        m2 = jnp.max(jnp.where(lane == i1, neg, vals), axis=-1, keepdims=True)
        gs.append(m1 + m2)
    allowed = jnp.zeros((tile, N_EXPERTS), F32)
    for g in range(N_GROUPS):
        ahead = jnp.zeros((tile, 1), F32)
        for o in range(N_GROUPS):
            if o == g:
                continue
            beats = (gs[o] > gs[g]) | ((gs[o] == gs[g]) & (o < g))
            ahead = ahead + jnp.where(beats, 1.0, 0.0)
        sel = jnp.where(ahead < TOPK_GROUPS, 1.0, 0.0)
        allowed = jnp.where(grp == g, sel, allowed)
    masked = jnp.where(allowed > 0.5, biased, neg)

    member = jnp.zeros((tile, N_EXPERTS), F32)
    idxs, ws = [], []
    for _ in range(TOP_K):
        _, ik = first_argmax(masked)
        hit = lane == ik
        ws.append(jnp.sum(jnp.where(hit, scores, 0.0), axis=-1, keepdims=True))
        masked = jnp.where(hit, neg, masked)
        member = jnp.where(hit, 1.0, member)
        idxs.append(ik)
    wsum = ws[0]
    for w in ws[1:]:
        wsum = wsum + w

    ri = lax.broadcasted_iota(jnp.int32, (tile, tile), 0)
    ci = lax.broadcasted_iota(jnp.int32, (tile, tile), 1)
    before = jnp.where(ci < ri, 1.0, 0.0).astype(BF16)
    rank_full = jnp.dot(before, member.astype(BF16), preferred_element_type=F32) + base_ref[...]
    base_ref[...] = base_ref[...] + jnp.sum(member, axis=0, keepdims=True)
    cnt_ref[...] = base_ref[...]

    slot = lax.broadcasted_iota(jnp.int32, (tile, TOP_K), 1)
    idx_o = jnp.zeros((tile, TOP_K), F32)
    wts_o = jnp.zeros((tile, TOP_K), F32)
    rank_o = jnp.zeros((tile, TOP_K), F32)
    for k in range(TOP_K):
        rk = jnp.sum(jnp.where(lane == idxs[k], rank_full, 0.0), axis=-1, keepdims=True)
        idx_o = jnp.where(slot == k, idxs[k], idx_o)
        wts_o = jnp.where(slot == k, ROUTE_SCALE * ws[k] / wsum, wts_o)
        rank_o = jnp.where(slot == k, rk, rank_o)
    idx_ref[...] = idx_o.astype(jnp.int32)
    wts_ref[...] = wts_o
    rank_ref[...] = rank_o.astype(jnp.int32)


def _route(logits, b_router):
    t = logits.shape[0]
    tile = ROW_TILE
    tok = lambda w: pl.BlockSpec((tile, w), lambda i: (i, 0))
    return pl.pallas_call(
        functools.partial(_route_kernel, tile=tile),
        grid=(t // tile,),
        in_specs=[tok(N_EXPERTS), pl.BlockSpec((1, N_EXPERTS), lambda i: (0, 0))],
        out_specs=[tok(TOP_K), tok(TOP_K), tok(TOP_K), pl.BlockSpec((1, N_EXPERTS), lambda i: (0, 0))],
        out_shape=[jax.ShapeDtypeStruct((t, TOP_K), jnp.int32),
                   jax.ShapeDtypeStruct((t, TOP_K), F32),
                   jax.ShapeDtypeStruct((t, TOP_K), jnp.int32),
                   jax.ShapeDtypeStruct((1, N_EXPERTS), F32)],
        scratch_shapes=[pltpu.VMEM((1, N_EXPERTS), F32)],
        compiler_params=_cparams("arbitrary"),
        name="route",
    )(logits, b_router.reshape(1, N_EXPERTS))


def _dispatch_kernel(dest_ref, h_ref, xs_in_ref, xs_ref, sem, *, tile):
    del xs_in_ref

    def row_copy(r, k):
        d = dest_ref[0, 0, r * TOP_K + k]
        return pltpu.make_async_copy(h_ref.at[pl.ds(r, 1)], xs_ref.at[pl.ds(d, 1)], sem)

    def issue(r, carry):
        for k in range(TOP_K):
            row_copy(r, k).start()
        return carry

    def drain(r, carry):
        for k in range(TOP_K):
            row_copy(r, k).wait()
        return carry

    lax.fori_loop(0, tile, issue, 0)
    lax.fori_loop(0, tile, drain, 0)


def _dispatch(h2, dest, xs_zero):
    t = h2.shape[0]
    tile = ROW_TILE
    dest3 = dest.reshape(t // tile, 1, tile * TOP_K)
    return pl.pallas_call(
        functools.partial(_dispatch_kernel, tile=tile),
        grid=(t // tile,),
        in_specs=[pl.BlockSpec((1, 1, tile * TOP_K), lambda i: (i, 0, 0), memory_space=pltpu.SMEM),
                  pl.BlockSpec((tile, D), lambda i: (i, 0)),
                  pl.BlockSpec(memory_space=pl.ANY)],
        out_specs=pl.BlockSpec(memory_space=pl.ANY),
        out_shape=jax.ShapeDtypeStruct(xs_zero.shape, F32),
        scratch_shapes=[pltpu.SemaphoreType.DMA(())],
        input_output_aliases={2: 0},
        compiler_params=_cparams("arbitrary"),
        name="dispatch",
    )(dest3, h2, xs_zero)


def _expert_kernel(be_ref, nu_ref, xs_ref, wg_ref, wu_ref, wd_ref, ys_ref, wgb, wub, wdb):
    i = pl.program_id(0)
    changed = (i == 0) | (be_ref[i] != be_ref[jnp.maximum(i - 1, 0)])

    @pl.when(changed)
    def _():
        wgb[...] = wg_ref[0].astype(BF16)
        wub[...] = wu_ref[0].astype(BF16)
        wdb[...] = wd_ref[0].astype(BF16)

    @pl.when(i < nu_ref[0])
    def _():
        x = xs_ref[...].astype(BF16)
        g = jnp.dot(x, wgb[...], preferred_element_type=F32)
        u = jnp.dot(x, wub[...], preferred_element_type=F32)
        h = (g * _sigmoid(g)) * u
        ys_ref[...] = jnp.dot(h.astype(BF16), wdb[...], preferred_element_type=F32)


def _experts(xs, blk_expert, n_used, w_gate, w_up, w_down):
    n_blocks = xs.shape[0] // MOE_ROWS
    rows = lambda i, be, nu: (jnp.minimum(i, nu[0] - 1), 0)
    return pl.pallas_call(
        _expert_kernel,
        grid_spec=pltpu.PrefetchScalarGridSpec(
            num_scalar_prefetch=2,
            grid=(n_blocks,),
            in_specs=[pl.BlockSpec((MOE_ROWS, D), rows),
                      pl.BlockSpec((1, D, EXPERT_HIDDEN), lambda i, be, nu: (be[i], 0, 0)),
                      pl.BlockSpec((1, D, EXPERT_HIDDEN), lambda i, be, nu: (be[i], 0, 0)),
                      pl.BlockSpec((1, EXPERT_HIDDEN, D), lambda i, be, nu: (be[i], 0, 0))],
            out_specs=pl.BlockSpec((MOE_ROWS, D), rows),
            scratch_shapes=[pltpu.VMEM((D, EXPERT_HIDDEN), BF16),
                            pltpu.VMEM((D, EXPERT_HIDDEN), BF16),
                            pltpu.VMEM((EXPERT_HIDDEN, D), BF16)]),
        out_shape=jax.ShapeDtypeStruct(xs.shape, F32),
        compiler_params=_cparams("arbitrary"),
        name="experts",
    )(blk_expert, n_used, xs, w_gate, w_up, w_down)


COMBINE_TILE = 128


def _combine_kernel(dest_ref, x_ref, h_ref, wts_ref, mods_ref, wsg_ref, wsu_ref, wsd_ref, gf_ref,
                    ys_ref, o_ref, buf, sem, *, tiles_per_batch, ctx_tile0, final):
    i = pl.program_id(0)
    tile = COMBINE_TILE

    def row_copy(r, k):
        d = dest_ref[0, 0, r * TOP_K + k]
        return pltpu.make_async_copy(ys_ref.at[pl.ds(d, 1)], buf.at[k, pl.ds(r, 1)], sem)

    def issue(r, carry):
        for k in range(TOP_K):
            row_copy(r, k).start()
        return carry

    def drain(r, carry):
        for k in range(TOP_K):
            row_copy(r, k).wait()
        return carry

    lax.fori_loop(0, tile, issue, 0)

    b = i // tiles_per_batch
    is_ctx = (i % tiles_per_batch) >= ctx_tile0
    row = jnp.where(is_ctx, 0, b + 1)
    g2 = mods_ref[pl.ds(row, 1), 5 * D:6 * D]
    h = h_ref[...].astype(BF16)
    sg = jnp.dot(h, wsg_ref[...], preferred_element_type=F32)
    su = jnp.dot(h, wsu_ref[...], preferred_element_type=F32)
    y = _bdot((sg * _sigmoid(sg)) * su, wsd_ref[...])

    lax.fori_loop(0, tile, drain, 0)
    w = wts_ref[...]
    for k in range(TOP_K):
        y = y + w[:, k:k + 1] * buf[k]
    out = x_ref[...] + g2 * y
    if final:
        out = _rms(out, gf_ref[...])
    o_ref[...] = out


def _combine(x_new, h2, wts, dest, ys, mods, ws_gate, ws_up, ws_down, g_final, n_lat, nt, final):
    t = x_new.shape[0]
    tile = COMBINE_TILE
    dest3 = dest.reshape(t // tile, 1, tile * TOP_K)
    tok = lambda w: pl.BlockSpec((tile, w), lambda i: (i, 0))
    const = lambda shape: pl.BlockSpec(shape, lambda i: (0,) * len(shape))
    return pl.pallas_call(
        functools.partial(_combine_kernel, tiles_per_batch=nt // tile, ctx_tile0=n_lat // tile,
                          final=final),
        grid=(t // tile,),
        in_specs=[pl.BlockSpec((1, 1, tile * TOP_K), lambda i: (i, 0, 0), memory_space=pltpu.SMEM),
                  tok(D), tok(D), tok(TOP_K), const((8, 6 * D)),
                  const((D, EXPERT_HIDDEN)), const((D, EXPERT_HIDDEN)), const((EXPERT_HIDDEN, D)),
                  const((1, D)), pl.BlockSpec(memory_space=pl.ANY)],
        out_specs=tok(D),
        out_shape=jax.ShapeDtypeStruct((t, D), F32),
        scratch_shapes=[pltpu.VMEM((TOP_K, tile, D), F32), pltpu.SemaphoreType.DMA(())],
        compiler_params=_cparams("arbitrary"),
        name="combine",
    )(dest3, x_new, h2, wts, mods, ws_gate, ws_up, ws_down, g_final.reshape(1, D), ys)


def _block_diag_heads(w):
    out = jnp.zeros((2, B_WIDTH, B_WIDTH), w.dtype)
    for h in range(B_HEADS):
        s = slice(h * B_HEAD_DIM, (h + 1) * B_HEAD_DIM)
        out = out.at[:, s, s].set(w[:, h])
    return out


def _moe(x_new, h2, logits, mods, b_router, w_gate, w_up, w_down, ws_gate, ws_up, ws_down,
         g_final, n_lat, nt, final):
    t = h2.shape[0]
    idx, wts, rank, counts = _route(logits, b_router)
    counts = counts.reshape(N_EXPERTS).astype(jnp.int32)
    padded = (counts + MOE_ROWS - 1) // MOE_ROWS * MOE_ROWS
    pend = jnp.cumsum(padded)
    pstart = pend - padded
    dest = pstart[idx] + rank
    n_blocks = -(-(t * TOP_K) // MOE_ROWS) + N_EXPERTS
    blk_expert = jnp.minimum(
        jnp.searchsorted(pend, jnp.arange(n_blocks, dtype=jnp.int32) * MOE_ROWS, side='right'),
        N_EXPERTS - 1).astype(jnp.int32)
    n_used = (pend[-1:] // MOE_ROWS).astype(jnp.int32)
    xs = _dispatch(h2, dest, jnp.zeros((n_blocks * MOE_ROWS, D), F32))
    ys = _experts(xs, blk_expert, n_used, w_gate, w_up, w_down)
    return _combine(x_new, h2, wts, dest, ys, mods, ws_gate.astype(BF16), ws_up.astype(BF16),
                    ws_down.astype(BF16), g_final, n_lat, nt, final)


def _layer(x_all, mods, p, lbc, n_lat, lb_zero, final):
    bsz, nt, _ = x_all.shape
    q, zf, zb, v, g, xb, yb, u = _inproj(x_all, mods, p['norm_mix'], p['w_in'].astype(BF16), n_lat)
    o_f = _hgrn2(q, zf, v, lbc, n_lat, reverse=False, lb_zero=lb_zero)
    a_all = _hgrn2(q, zb, v, lbc, n_lat, reverse=True, lb_zero=lb_zero, oprev=o_f, g=g,
                   gain=p['hg_norm'])
    hb_lat, hb_ctx = _rglru(xb, n_lat, p['lru_conv_w'], p['lru_conv_b'],
                            _block_diag_heads(p['lru_wa']).astype(BF16),
                            _block_diag_heads(p['lru_wx']).astype(BF16),
                            p['lru_ba'], p['lru_bx'], p['lru_lam'])
    c_lat = _fourier_latent(u, n_lat)
    c_ctx = _fourier_ctx(u, n_lat)
    wr = p['w_router']
    wr_hi = wr.astype(BF16)
    wr_lo = (wr - wr_hi.astype(F32)).astype(BF16)
    x_new, h2, logits = _outproj(x_all, a_all, hb_lat, hb_ctx, yb, c_lat, c_ctx, mods, p['norm_ffn'],
                                 p['w_out'].astype(BF16), wr_hi, wr_lo, n_lat)
    t = bsz * nt
    out = _moe(x_new.reshape(t, D), h2.reshape(t, D), logits.reshape(t, N_EXPERTS), mods,
               p['b_router'], p['w_exp_gate'], p['w_exp_up'], p['w_exp_down'],
               p['w_sh_gate'], p['w_sh_up'], p['w_sh_down'], p['norm_final'], n_lat, nt, final)
    return out.reshape(bsz, nt, D)


def _forward(x, c, ctx, c_ctx, params):
    bsz, n_lat, _ = x.shape
    n_layers = params['w_mod'].shape[0]
    cc = jnp.zeros((8, D), F32).at[0].set(c_ctx).at[1:1 + bsz].set(c)
    mods_all = _modulation(cc, params['w_mod'], params['b_mod'])
    lb_all = jnp.cumsum(jax.nn.softmax(params['hg_lb'].astype(F32), axis=0), axis=0)
    lb_all = lb_all - lb_all[0:1]
    x_all = jnp.concatenate([x, ctx], axis=1)
    per_layer = ('norm_mix', 'norm_ffn', 'w_in', 'hg_norm', 'lru_conv_w', 'lru_conv_b', 'lru_wa',
                 'lru_ba', 'lru_wx', 'lru_bx', 'lru_lam', 'w_out', 'w_router', 'b_router',
                 'w_exp_gate', 'w_exp_up', 'w_exp_down', 'w_sh_gate', 'w_sh_up', 'w_sh_down')
    for l in range(n_layers):
        p = {k: params[k][l] for k in per_layer}
        p['norm_final'] = params['norm_final']
        lb = lb_all[l]
        lbc = jnp.zeros((8, A_WIDTH), F32).at[0].set(jnp.log(lb)).at[1].set(jnp.log1p(-lb)).at[2].set(1.0 - lb)
        x_all = _layer(x_all, mods_all[l], p, lbc, n_lat, lb_zero=(l == 0), final=(l == n_layers - 1))
    return x_all[:, :n_lat]


def kernel(x, c, ctx, c_ctx, w_mod, b_mod, norm_mix, norm_ffn, w_in, hg_lb, hg_norm, lru_conv_w,
           lru_conv_b, lru_wa, lru_ba, lru_wx, lru_bx, lru_lam, w_out, w_router, b_router,
           w_exp_gate, w_exp_up, w_exp_down, w_sh_gate, w_sh_up, w_sh_down, norm_final):
    params = dict(w_mod=w_mod, b_mod=b_mod, norm_mix=norm_mix, norm_ffn=norm_ffn, w_in=w_in,
                  hg_lb=hg_lb, hg_norm=hg_norm, lru_conv_w=lru_conv_w, lru_conv_b=lru_conv_b,
                  lru_wa=lru_wa, lru_ba=lru_ba, lru_wx=lru_wx, lru_bx=lru_bx, lru_lam=lru_lam,
                  w_out=w_out, w_router=w_router, b_router=b_router, w_exp_gate=w_exp_gate,
                  w_exp_up=w_exp_up, w_exp_down=w_exp_down, w_sh_gate=w_sh_gate, w_sh_up=w_sh_up,
                  w_sh_down=w_sh_down, norm_final=norm_final)
    return _forward(x, c, ctx, c_ctx, params)
```
